```python
import math
import jax
import jax.numpy as jnp
from jax import lax
import numpy as np

D_MODEL = 1024
BATCH = 8
SEQ = 2048
DEPTH = 2
DEC_BATCH = 128
DEC_SEQ = 4
PAST_LEN = 2048
PAGE_SIZE = 128

D_RNN = D_MODEL
N_RNN_BLOCKS = 8
RNN_BLOCK = D_RNN // N_RNN_BLOCKS
CONV_WIDTH = 4
RG_C = 8.0
N_HEADS = 8
HEAD_DIM = D_MODEL // N_HEADS
D_ATT = N_HEADS * HEAD_DIM
Q_BLOCK = 128
SB_BIAS_INIT = -6.0
N_GROUPS = 4
EXPERTS_PER_GROUP = 4
N_EXPERTS = N_GROUPS * EXPERTS_PER_GROUP
TOP_K = 2
D_EXPERT = D_MODEL // 2
N_MOD = 6
D_IN = 2 * D_RNN + 3 * D_ATT + 2 * D_MODEL
EPS = 1e-6

kernel_name = 'hawk_stickbreak_hmoe_adaln_step'


def rms_norm(x, g):
    xf = x.astype(jnp.float32)
    y = xf * lax.rsqrt(jnp.mean(xf * xf, axis=-1, keepdims=True) + EPS)
    return (y * g.astype(jnp.float32)).astype(x.dtype)


def modulate(x, shift, scale):
    return x * (1 + scale[:, None, :]) + shift[:, None, :]


def rg_lru_branch(xa, conv_state, h0, conv_w, conv_b, w_a, b_a, w_i, b_i, lam):
    bsz, t_len, _ = xa.shape
    xc = jnp.concatenate([conv_state.astype(xa.dtype), xa], axis=1)
    new_conv = xc[:, xc.shape[1] - (CONV_WIDTH - 1):]
    u = conv_b
    for j in range(CONV_WIDTH):
        u = u + conv_w[j] * xc[:, j:j + t_len]
    ub = u.reshape(bsz, t_len, N_RNN_BLOCKS, RNN_BLOCK)
    r = jax.nn.sigmoid(jnp.einsum('btnc,ncd->btnd', ub, w_a).reshape(bsz, t_len, D_RNN) + b_a)
    i = jax.nn.sigmoid(jnp.einsum('btnc,ncd->btnd', ub, w_i).reshape(bsz, t_len, D_RNN) + b_i)
    log_a = RG_C * r.astype(jnp.float32) * jax.nn.log_sigmoid(lam.astype(jnp.float32))
    a = jnp.exp(log_a)
    b = jnp.sqrt(-jnp.expm1(2.0 * log_a)) * (i * u).astype(jnp.float32)

    def step(h, ab):
        a_t, b_t = ab
        h = a_t * h + b_t
        return h, h

    h_last, hs = lax.scan(step, h0.astype(jnp.float32), (jnp.swapaxes(a, 0, 1), jnp.swapaxes(b, 0, 1)))
    return jnp.swapaxes(hs, 0, 1).astype(xa.dtype), h_last.astype(xa.dtype), new_conv


def stick_breaking(q, k, v, sb_bias, q_offset):
    t_q = q.shape[1]
    t_k = k.shape[1]
    scale = 1.0 / math.sqrt(HEAD_DIM)
    bias = sb_bias.astype(jnp.float32)[None, :, None, None]
    outs = []
    for s0 in range(0, t_q, Q_BLOCK):
        s1 = min(s0 + Q_BLOCK, t_q)
        k_end = min(t_k, q_offset + s1)
        z = jnp.einsum('bqhd,bkhd->bhqk', q[:, s0:s1], k[:, :k_end]).astype(jnp.float32) * scale + bias
        q_pos = q_offset + jnp.arange(s0, s1)
        k_pos = jnp.arange(k_end)
        causal = k_pos[None, :] < q_pos[:, None]
        log_not = jnp.where(causal, jax.nn.log_sigmoid(-z), 0.0)
        excl = lax.cumsum(log_not, axis=3, reverse=True) - log_not
        att = jnp.where(causal, jnp.exp(jax.nn.log_sigmoid(z) + excl), 0.0)
        outs.append(jnp.einsum('bhqk,bkhd->bqhd', att.astype(v.dtype), v[:, :k_end]))
    return jnp.concatenate(outs, axis=1)


def hier_moe(x, w_rg, b_rg, w_re, b_re, w_gate, w_up, w_down):
    shp = x.shape
    t = x.reshape(-1, D_MODEL)
    n = t.shape[0]
    p_group = jax.nn.softmax((t @ w_rg + b_rg).astype(jnp.float32), axis=-1)
    p_top, g_idx = lax.top_k(p_group, 1)
    le = (t @ w_re + b_re).astype(jnp.float32).reshape(n, N_GROUPS, EXPERTS_PER_GROUP)
    le_g = jnp.take_along_axis(le, g_idx[:, :, None], axis=1)[:, 0]
    p_in = jax.nn.softmax(le_g, axis=-1)
    w_sel, e_local = lax.top_k(p_in, TOP_K)
    w_sel = w_sel / jnp.sum(w_sel, axis=-1, keepdims=True) * p_top
    e_idx = g_idx * EXPERTS_PER_GROUP + e_local
    gates = jnp.sum(jax.nn.one_hot(e_idx, N_EXPERTS, dtype=jnp.float32) * w_sel[..., None], axis=1)
    hdn = jax.nn.silu(jnp.einsum('nd,edf->nef', t, w_gate)) * jnp.einsum('nd,edf->nef', t, w_up)
    hdn = hdn * gates.astype(t.dtype)[..., None]
    y = jnp.einsum('nef,efd->nd', hdn, w_down)
    return y.reshape(shp)


def decoder_layer(x, c, conv_state, h_state, k_past, v_past, p):
    bsz, t_len, _ = x.shape
    mod = (jax.nn.silu(c) @ p['w_mod'] + p['b_mod']).reshape(bsz, N_MOD, D_MODEL)
    shift1, scale1, gate1 = mod[:, 0], mod[:, 1], mod[:, 2]
    shift2, scale2, gate2 = mod[:, 3], mod[:, 4], mod[:, 5]

    h = modulate(rms_norm(x, p['norm1_g']), shift1, scale1)
    proj = h @ p['w_in']
    splits = [D_RNN, 2 * D_RNN, 2 * D_RNN + D_ATT, 2 * D_RNN + 2 * D_ATT, 2 * D_RNN + 3 * D_ATT]
    xa, ya, q, k, v, g_logits = jnp.split(proj, splits, axis=-1)

    ha, h_last, new_conv = rg_lru_branch(
        xa, conv_state, h_state, p['conv_w'], p['conv_b'],
        p['w_rg_a'], p['b_rg_a'], p['w_rg_i'], p['b_rg_i'], p['lru_lambda'])
    out_a = (ha * jax.nn.gelu(ya)) @ p['w_branch_a']

    q = q.reshape(bsz, t_len, N_HEADS, HEAD_DIM)
    k = k.reshape(bsz, t_len, N_HEADS, HEAD_DIM)
    v = v.reshape(bsz, t_len, N_HEADS, HEAD_DIM)
    if k_past is None:
        k_all, v_all, offset = k, v, 0
    else:
        k_all = jnp.concatenate([k_past.astype(k.dtype), k], axis=1)
        v_all = jnp.concatenate([v_past.astype(v.dtype), v], axis=1)
        offset = k_past.shape[1]
    o = stick_breaking(q, k_all, v_all, p['b_sb'], offset).reshape(bsz, t_len, D_ATT)
    out_b = o @ p['w_branch_b']

    g_a, g_b = jnp.split(jax.nn.sigmoid(g_logits), 2, axis=-1)
    merged = (g_a * out_a + g_b * out_b) @ p['w_out']
    x = x + gate1[:, None, :] * merged

    h2 = modulate(rms_norm(x, p['norm2_g']), shift2, scale2)
    x = x + gate2[:, None, :] * hier_moe(
        h2, p['w_router_group'], p['b_router_group'], p['w_router_expert'], p['b_router_expert'],
        p['w_exp_gate'], p['w_exp_up'], p['w_exp_down'])
    return x, k, v, h_last, new_conv


def setup_inputs(seed: int = 0) -> dict:
    key = jax.random.key(seed)
    ks = jax.random.split(key, 40)
    f32 = jnp.float32
    n_pages = PAST_LEN // PAGE_SIZE
    n_used = DEC_BATCH * n_pages
    n_pool = n_used + max(1, n_used // 4)

    def nrm(k, shape, scale):
        return jax.random.normal(k, shape, f32) * scale

    page_table = jax.random.permutation(ks[0], n_pool)[:n_used].reshape(DEC_BATCH, n_pages).astype(jnp.int32)
    u = jax.random.uniform(ks[1], (DEPTH, D_RNN), f32, 0.9, 0.999)
    s = u ** (1.0 / RG_C)
    lru_lambda = jnp.log(s) - jnp.log1p(-s)
    dm = D_MODEL ** -0.5
    return {
        'x_prompt': nrm(ks[2], (BATCH, SEQ, D_MODEL), 1.0),
        'x_sample': nrm(ks[3], (DEC_BATCH, DEC_SEQ, D_MODEL), 1.0),
        'c_prompt': nrm(ks[4], (BATCH, D_MODEL), 1.0),
        'c_sample': nrm(ks[5], (DEC_BATCH, D_MODEL), 1.0),
        'cache_k': nrm(ks[6], (DEPTH, n_pool, PAGE_SIZE, N_HEADS, HEAD_DIM), 1.0),
        'cache_v': nrm(ks[7], (DEPTH, n_pool, PAGE_SIZE, N_HEADS, HEAD_DIM), 1.0),
        'state_h': nrm(ks[8], (DEPTH, DEC_BATCH, D_RNN), 0.5),
        'state_conv': nrm(ks[9], (DEPTH, DEC_BATCH, CONV_WIDTH - 1, D_RNN), 1.0),
        'page_table': page_table,
        'w_mod': nrm(ks[10], (DEPTH, D_MODEL, N_MOD * D_MODEL), 0.5 * dm),
        'b_mod': nrm(ks[11], (DEPTH, N_MOD * D_MODEL), 0.02),
        'norm1_g': 1.0 + nrm(ks[12], (DEPTH, D_MODEL), 0.02),
        'norm2_g': 1.0 + nrm(ks[13], (DEPTH, D_MODEL), 0.02),
        'w_in': nrm(ks[14], (DEPTH, D_MODEL, D_IN), dm),
        'b_sb': SB_BIAS_INIT + nrm(ks[32], (DEPTH, N_HEADS), 0.5),
        'conv_w': nrm(ks[15], (DEPTH, CONV_WIDTH, D_RNN), CONV_WIDTH ** -0.5),
        'conv_b': nrm(ks[16], (DEPTH, D_RNN), 0.02),
        'w_rg_a': nrm(ks[17], (DEPTH, N_RNN_BLOCKS, RNN_BLOCK, RNN_BLOCK), RNN_BLOCK ** -0.5),
        'b_rg_a': nrm(ks[18], (DEPTH, D_RNN), 0.02),
        'w_rg_i': nrm(ks[19], (DEPTH, N_RNN_BLOCKS, RNN_BLOCK, RNN_BLOCK), RNN_BLOCK ** -0.5),
        'b_rg_i': nrm(ks[20], (DEPTH, D_RNN), 0.02),
        'lru_lambda': lru_lambda,
        'w_branch_a': nrm(ks[21], (DEPTH, D_RNN, D_MODEL), D_RNN ** -0.5),
        'w_branch_b': nrm(ks[22], (DEPTH, D_ATT, D_MODEL), D_ATT ** -0.5),
        'w_out': nrm(ks[23], (DEPTH, D_MODEL, D_MODEL), dm),
        'w_router_group': nrm(ks[24], (DEPTH, D_MODEL, N_GROUPS), dm),
        'b_router_group': nrm(ks[25], (DEPTH, N_GROUPS), 0.01),
        'w_router_expert': nrm(ks[26], (DEPTH, D_MODEL, N_EXPERTS), dm),
        'b_router_expert': nrm(ks[27], (DEPTH, N_EXPERTS), 0.01),
        'w_exp_gate': nrm(ks[28], (DEPTH, N_EXPERTS, D_MODEL, D_EXPERT), dm),
        'w_exp_up': nrm(ks[29], (DEPTH, N_EXPERTS, D_MODEL, D_EXPERT), dm),
        'w_exp_down': nrm(ks[30], (DEPTH, N_EXPERTS, D_EXPERT, D_MODEL), D_EXPERT ** -0.5),
        'final_g': 1.0 + nrm(ks[31], (D_MODEL,), 0.02),
    }


def reference(x_prompt, x_sample, c_prompt, c_sample, cache_k, cache_v, state_h, state_conv, page_table,
              w_mod, b_mod, norm1_g, norm2_g, w_in, b_sb, conv_w, conv_b, w_rg_a, b_rg_a, w_rg_i, b_rg_i,
              lru_lambda, w_branch_a, w_branch_b, w_out, w_router_group, b_router_group,
              w_router_expert, b_router_expert, w_exp_gate, w_exp_up, w_exp_down, final_g):
    bp = x_prompt.shape[0]
    bs, n_pages = page_table.shape
    past_len = n_pages * cache_k.shape[2]
    xp, xs = x_prompt, x_sample
    kp_l, vp_l, hp_l, cp_l = [], [], [], []
    ks_l, vs_l, hs_l, cs_l = [], [], [], []
    for l in range(DEPTH):
        lp = {
            'w_mod': w_mod[l], 'b_mod': b_mod[l],
            'norm1_g': norm1_g[l], 'norm2_g': norm2_g[l],
            'w_in': w_in[l], 'b_sb': b_sb[l], 'conv_w': conv_w[l], 'conv_b': conv_b[l],
            'w_rg_a': w_rg_a[l], 'b_rg_a': b_rg_a[l], 'w_rg_i': w_rg_i[l], 'b_rg_i': b_rg_i[l],
            'lru_lambda': lru_lambda[l],
            'w_branch_a': w_branch_a[l], 'w_branch_b': w_branch_b[l], 'w_out': w_out[l],
            'w_router_group': w_router_group[l], 'b_router_group': b_router_group[l],
            'w_router_expert': w_router_expert[l], 'b_router_expert': b_router_expert[l],
            'w_exp_gate': w_exp_gate[l], 'w_exp_up': w_exp_up[l], 'w_exp_down': w_exp_down[l],
        }
        zero_conv = jnp.zeros((bp, CONV_WIDTH - 1, D_RNN), xp.dtype)
        zero_h = jnp.zeros((bp, D_RNN), xp.dtype)
        xp, kp, vp, hp, cp = decoder_layer(xp, c_prompt, zero_conv, zero_h, None, None, lp)
        k_past = cache_k[l][page_table].reshape(bs, past_len, N_HEADS, HEAD_DIM)
        v_past = cache_v[l][page_table].reshape(bs, past_len, N_HEADS, HEAD_DIM)
        xs, ksm, vsm, hsm, csm = decoder_layer(xs, c_sample, state_conv[l], state_h[l], k_past, v_past, lp)
        kp_l.append(kp)
        vp_l.append(vp)
        hp_l.append(hp)
        cp_l.append(cp)
        ks_l.append(ksm)
        vs_l.append(vsm)
        hs_l.append(hsm)
        cs_l.append(csm)
    y_prompt = rms_norm(xp, final_g)
    y_sample = rms_norm(xs, final_g)
    return (y_prompt, y_sample,
            jnp.stack(kp_l), jnp.stack(vp_l), jnp.stack(hp_l), jnp.stack(cp_l),
            jnp.stack(ks_l), jnp.stack(vs_l), jnp.stack(hs_l), jnp.stack(cs_l))
```

```python
import functools
import math

import jax
import jax.numpy as jnp
from jax import lax
from jax.experimental import pallas as pl
from jax.experimental.pallas import tpu as pltpu

F32 = jnp.float32
BF16 = jnp.bfloat16

D_MODEL = 1024
N_HEADS = 8
HEAD_DIM = 128
N_RNN_BLOCKS = 8
RNN_BLOCK = 128
CONV_WIDTH = 4
RG_C = 8.0
N_GROUPS = 4
EXPERTS_PER_GROUP = 4
N_EXPERTS = 16
D_EXPERT = 512
N_MOD = 6
N_PROJ = 7
EPS = 1e-6
ATT_SCALE = 1.0 / math.sqrt(HEAD_DIM)
LANES = 128
NEG_BIG = -3.0e38
MIB = 1024 * 1024


def _cparams(sem, vmem_mib):
    return pltpu.CompilerParams(dimension_semantics=sem, vmem_limit_bytes=vmem_mib * MIB)


def _sigmoid(x):
    return 1.0 / (1.0 + jnp.exp(-x))


def _softplus(x):
    return jnp.maximum(x, 0.0) + jnp.log(1.0 + jnp.exp(-jnp.abs(x)))


def _rms_mod(x, g, shift, scale):
    ms = jnp.mean(x * x, axis=-1, keepdims=True)
    y = x * lax.rsqrt(ms + EPS) * g
    return y * (1.0 + scale) + shift


def _mod_spec(arr, tiles_per_row):
    bm, r, d = arr.shape
    if bm == 1:
        return pl.BlockSpec((None, r, d), lambda i, *_: (0, 0, 0))
    return pl.BlockSpec((None, r, d), lambda i, *_: (i // tiles_per_row, 0, 0))


def _mod_kernel(c_ref, w_ref, b_ref, o_ref):
    c = c_ref[...]
    s = c * _sigmoid(c)
    o_ref[...] = jnp.dot(s.astype(BF16), w_ref[...], preferred_element_type=F32) + b_ref[...]


def _mod_call(c_all, w_mod, b_mod):
    nb = c_all.shape[0]
    return pl.pallas_call(
        _mod_kernel,
        grid=(N_MOD,),
        in_specs=[
            pl.BlockSpec((nb, D_MODEL), lambda j: (0, 0)),
            pl.BlockSpec((D_MODEL, D_MODEL), lambda j: (0, j)),
            pl.BlockSpec((1, D_MODEL), lambda j: (0, j)),
        ],
        out_specs=pl.BlockSpec((nb, D_MODEL), lambda j: (0, j)),
        out_shape=jax.ShapeDtypeStruct((nb, N_MOD * D_MODEL), F32),
        compiler_params=_cparams(("arbitrary",), 32),
        name="mod",
    )(c_all, w_mod, b_mod)


def _inproj_kernel(x_ref, shift_ref, scale_ref, g_ref, w_ref, o_ref, h_scr):
    @pl.when(pl.program_id(1) == 0)
    def _():
        h = _rms_mod(x_ref[...], g_ref[...], shift_ref[...], scale_ref[...])
        h_scr[...] = h.astype(BF16)

    o_ref[...] = jnp.dot(h_scr[...], w_ref[...], preferred_element_type=F32)


def _inproj_call(x, shift, scale, g, w_in, tm, tiles_per_row):
    n = x.shape[0]
    return pl.pallas_call(
        _inproj_kernel,
        grid=(n // tm, N_PROJ),
        in_specs=[
            pl.BlockSpec((tm, D_MODEL), lambda i, j: (i, 0)),
            _mod_spec(shift, tiles_per_row),
            _mod_spec(scale, tiles_per_row),
            pl.BlockSpec((1, D_MODEL), lambda i, j: (0, 0)),
            pl.BlockSpec((D_MODEL, D_MODEL), lambda i, j: (0, j)),
        ],
        out_specs=pl.BlockSpec((None, tm, D_MODEL), lambda i, j: (j, i, 0)),
        out_shape=jax.ShapeDtypeStruct((N_PROJ, n, D_MODEL), F32),
        scratch_shapes=[pltpu.VMEM((tm, D_MODEL), BF16)],
        compiler_params=_cparams(("arbitrary", "arbitrary"), 40),
        name="inproj",
    )(x, shift, scale, g, w_in)


def _log_sigmoid(x):
    return -_softplus(-x)


def _gelu_tanh(y):
    c = math.sqrt(2.0 / math.pi)
    return 0.5 * y * (1.0 + jnp.tanh(c * (y + 0.044715 * (y * y * y))))


def _rglru_gates(u, wai, ba, bi, lsl):
    ri = jnp.dot(u.astype(BF16), wai, preferred_element_type=F32)
    r = _sigmoid(ri[:, :RNN_BLOCK] + ba)
    i = _sigmoid(ri[:, RNN_BLOCK:] + bi)
    log_a = RG_C * r * lsl
    a = jnp.exp(log_a)
    b = jnp.sqrt(-jnp.tanh(log_a) * (a * a + 1.0)) * (i * u)
    return a, b


def _rglru_prompt_kernel(xa_ref, ya_ref, cw_ref, cb_ref, wai_ref, ba_ref, bi_ref, lam_ref,
                         hg_ref, hlast_ref, a_scr, b_scr, s_scr, halo_scr, h_scr, *, tc, nb):
    c = pl.program_id(0)

    @pl.when(c == 0)
    def _():
        h_scr[...] = jnp.zeros_like(h_scr)
        halo_scr[...] = jnp.zeros_like(halo_scr)

    lsl = _log_sigmoid(lam_ref[...])

    def per_batch(b, carry):
        s_scr[0:8, :] = halo_scr[b]
        s_scr[8:8 + tc, :] = xa_ref[b]
        halo_scr[b] = xa_ref[b, tc - 8:tc, :]
        row0 = pl.multiple_of(b * tc, tc)
        for n in range(N_RNN_BLOCKS):
            ls = slice(n * RNN_BLOCK, (n + 1) * RNN_BLOCK)
            u = cb_ref[:, ls]
            for j in range(CONV_WIDTH):
                off = 8 - (CONV_WIDTH - 1) + j
                u = u + cw_ref[j:j + 1, ls] * s_scr[off:off + tc, ls]
            a, bc = _rglru_gates(u, wai_ref[n], ba_ref[:, ls], bi_ref[:, ls], lsl[:, ls])
            a_scr[n, pl.ds(row0, tc), :] = a
            b_scr[n, pl.ds(row0, tc), :] = bc
        return carry

    lax.fori_loop(0, nb, per_batch, 0)

    def step(t, hs):
        out = []
        for n in range(N_RNN_BLOCKS):
            h = a_scr[n, pl.ds(t, nb, stride=tc), :] * hs[n] + b_scr[n, pl.ds(t, nb, stride=tc), :]
            b_scr[n, pl.ds(t, nb, stride=tc), :] = h
            out.append(h)
        return tuple(out)

    hs = lax.fori_loop(0, tc, step, tuple(h_scr[n] for n in range(N_RNN_BLOCKS)), unroll=8)
    for n in range(N_RNN_BLOCKS):
        h_scr[n] = hs[n]
        hlast_ref[:, n * RNN_BLOCK:(n + 1) * RNN_BLOCK] = hs[n]

    def per_batch_out(b, carry):
        row0 = pl.multiple_of(b * tc, tc)
        for n in range(N_RNN_BLOCKS):
            ls = slice(n * RNN_BLOCK, (n + 1) * RNN_BLOCK)
            hs = b_scr[n, pl.ds(row0, tc), :]
            hg_ref[b, :, ls] = (hs * _gelu_tanh(ya_ref[b, :, ls])).astype(BF16)
        return carry

    lax.fori_loop(0, nb, per_batch_out, 0)


def _rglru_prompt_call(proj, conv_w, conv_b, wai, b_a, b_i, lam, nb, t_len, tc):
    proj4 = proj.reshape(N_PROJ, nb, t_len, D_MODEL)
    vec = pl.BlockSpec((1, D_MODEL), lambda c: (0, 0))
    kern = functools.partial(_rglru_prompt_kernel, tc=tc, nb=nb)
    return pl.pallas_call(
        kern,
        grid=(t_len // tc,),
        in_specs=[
            pl.BlockSpec((None, nb, tc, D_MODEL), lambda c: (0, 0, c, 0)),
            pl.BlockSpec((None, nb, tc, D_MODEL), lambda c: (1, 0, c, 0)),
            pl.BlockSpec((CONV_WIDTH, D_MODEL), lambda c: (0, 0)),
            vec,
            pl.BlockSpec((N_RNN_BLOCKS, RNN_BLOCK, 2 * RNN_BLOCK), lambda c: (0, 0, 0)),
            vec, vec, vec,
        ],
        out_specs=[
            pl.BlockSpec((nb, tc, D_MODEL), lambda c: (0, c, 0)),
            pl.BlockSpec((nb, D_MODEL), lambda c: (0, 0)),
        ],
        out_shape=[
            jax.ShapeDtypeStruct((nb, t_len, D_MODEL), BF16),
            jax.ShapeDtypeStruct((nb, D_MODEL), F32),
        ],
        scratch_shapes=[
            pltpu.VMEM((N_RNN_BLOCKS, nb * tc, RNN_BLOCK), F32),
            pltpu.VMEM((N_RNN_BLOCKS, nb * tc, RNN_BLOCK), F32),
            pltpu.VMEM((8 + tc, D_MODEL), F32),
            pltpu.VMEM((nb, 8, D_MODEL), F32),
            pltpu.VMEM((N_RNN_BLOCKS, nb, RNN_BLOCK), F32),
        ],
        compiler_params=_cparams(("arbitrary",), 48),
        name="rglru_prompt",
    )(proj4, proj4, conv_w, conv_b, wai, b_a, b_i, lam)


def _rglru_sample_kernel(xa_ref, ya_ref, cs_ref, h0_ref, cw_ref, cb_ref, wai_ref, ba_ref, bi_ref, lam_ref,
                         hg_ref, hlast_ref, *, t_len):
    lsl = _log_sigmoid(lam_ref[...])
    xc = [cs_ref[j] for j in range(CONV_WIDTH - 1)] + [xa_ref[t] for t in range(t_len)]
    h = h0_ref[...]
    for t in range(t_len):
        u = cb_ref[...]
        for j in range(CONV_WIDTH):
            u = u + cw_ref[j:j + 1, :] * xc[t + j]
        a, bc = _rglru_gates(u, wai_ref[...], ba_ref[...], bi_ref[...], lsl)
        h = a * h + bc
        hg_ref[t] = (h * _gelu_tanh(ya_ref[t])).astype(BF16)
    hlast_ref[...] = h


def _rglru_sample_call(proj, cs_tm, h0, conv_w, conv_b, wai, b_a, b_i, lam, nb, t_len):
    proj4 = proj.reshape(N_PROJ, t_len, nb, D_MODEL)
    vec = pl.BlockSpec((1, RNN_BLOCK), lambda n: (0, n))
    kern = functools.partial(_rglru_sample_kernel, t_len=t_len)
    return pl.pallas_call(
        kern,
        grid=(N_RNN_BLOCKS,),
        in_specs=[
            pl.BlockSpec((None, t_len, nb, RNN_BLOCK), lambda n: (0, 0, 0, n)),
            pl.BlockSpec((None, t_len, nb, RNN_BLOCK), lambda n: (1, 0, 0, n)),
            pl.BlockSpec((CONV_WIDTH - 1, nb, RNN_BLOCK), lambda n: (0, 0, n)),
            pl.BlockSpec((nb, RNN_BLOCK), lambda n: (0, n)),
            pl.BlockSpec((CONV_WIDTH, RNN_BLOCK), lambda n: (0, n)),
            vec,
            pl.BlockSpec((None, RNN_BLOCK, 2 * RNN_BLOCK), lambda n: (n, 0, 0)),
            vec, vec, vec,
        ],
        out_specs=[
            pl.BlockSpec((t_len, nb, RNN_BLOCK), lambda n: (0, 0, n)),
            pl.BlockSpec((nb, RNN_BLOCK), lambda n: (0, n)),
        ],
        out_shape=[
            jax.ShapeDtypeStruct((t_len, nb, D_MODEL), BF16),
            jax.ShapeDtypeStruct((nb, D_MODEL), F32),
        ],
        compiler_params=_cparams(("arbitrary",), 32),
        name="rglru_sample",
    )(proj4, proj4, cs_tm, h0, conv_w, conv_b, wai, b_a, b_i, lam)


def _tri_rev(n):
    row = lax.broadcasted_iota(jnp.int32, (n, n), 0)
    col = lax.broadcasted_iota(jnp.int32, (n, n), 1)
    return jnp.where(row >= col, 1.0, 0.0).astype(BF16)


def _sb_block(q, kblk, vblk, bias, tri, causal, acc_ref, carry_ref):
    z = lax.dot_general(q, kblk, (((1,), (1,)), ((), ())), preferred_element_type=F32) + bias
    logn = -_softplus(z)
    if causal is not None:
        logn = jnp.where(causal, logn, 0.0)
    hi = logn.astype(BF16)
    lo = (logn - hi.astype(F32)).astype(BF16)
    cum = (jnp.dot(hi, tri, preferred_element_type=F32) + jnp.dot(lo, tri, preferred_element_type=F32)
           + carry_ref[...])
    carry_ref[...] += jnp.sum(logn, axis=-1, keepdims=True)
    att = jnp.exp(z + cum)
    if causal is not None:
        att = jnp.where(causal, att, 0.0)
    acc_ref[...] += jnp.dot(att.astype(BF16), vblk, preferred_element_type=F32)


def _attn_prompt_kernel(bias_ref, q_ref, k_ref, v_ref, o_ref, kbf, vbf, acc, carry, *, tq):
    h = pl.program_id(1)
    qi = pl.program_id(2)

    @pl.when(qi == 0)
    def _():
        kbf[...] = k_ref[...].astype(BF16)
        vbf[...] = v_ref[...].astype(BF16)

    q = (q_ref[...] * ATT_SCALE).astype(BF16)
    bias = bias_ref[h]
    tri = _tri_rev(tq)
    row = lax.broadcasted_iota(jnp.int32, (tq, tq), 0)
    col = lax.broadcasted_iota(jnp.int32, (tq, tq), 1)
    acc[...] = jnp.zeros_like(acc)
    carry[...] = jnp.zeros_like(carry)

    def block(kb, causal):
        k0 = pl.multiple_of(kb * tq, tq)
        _sb_block(q, kbf[pl.ds(k0, tq), :], vbf[pl.ds(k0, tq), :], bias, tri, causal, acc, carry)

    block(qi, col < row)

    def body(i, c):
        block(qi - 1 - i, None)
        return c

    lax.fori_loop(0, qi, body, 0)
    o_ref[...] = acc[...].astype(o_ref.dtype)


def _attn_prompt_call(proj, b_sb, nb, t_len, tq):
    proj4 = proj.reshape(N_PROJ, nb, t_len, D_MODEL)
    kern = functools.partial(_attn_prompt_kernel, tq=tq)
    kv = lambda j: pl.BlockSpec((None, None, t_len, HEAD_DIM), lambda b, h, qi: (j, b, 0, h))
    return pl.pallas_call(
        kern,
        grid=(nb, N_HEADS, t_len // tq),
        in_specs=[
            pl.BlockSpec(memory_space=pltpu.SMEM),
            pl.BlockSpec((None, None, tq, HEAD_DIM), lambda b, h, qi: (2, b, qi, h)),
            kv(3), kv(4),
        ],
        out_specs=pl.BlockSpec((None, tq, HEAD_DIM), lambda b, h, qi: (b, qi, h)),
        out_shape=jax.ShapeDtypeStruct((nb, t_len, D_MODEL), BF16),
        scratch_shapes=[
            pltpu.VMEM((t_len, HEAD_DIM), BF16),
            pltpu.VMEM((t_len, HEAD_DIM), BF16),
            pltpu.VMEM((tq, HEAD_DIM), F32),
            pltpu.VMEM((tq, 1), F32),
        ],
        compiler_params=_cparams(("arbitrary", "arbitrary", "arbitrary"), 32),
        name="attn_prompt",
    )(b_sb, proj4, proj4, proj4)


QROWS = 32
NEW_PAD = 8
PAGES_PER_STEP = 8


def _attn_decode_kernel(pt_ref, bias_ref, q_ref, kn_ref, vn_ref, *rest, page, t_new):
    npg = PAGES_PER_STEP
    kp, vp = rest[:npg], rest[npg:2 * npg]
    o_ref = rest[2 * npg]
    qbd, knew, vnew, acc, carry = rest[2 * npg + 1:]
    s = pl.program_id(1)
    tri = _tri_rev(page)
    bias = bias_ref[...]

    @pl.when(s == 0)
    def _():
        lane = lax.broadcasted_iota(jnp.int32, (N_HEADS, D_MODEL), 1)
        sub = lax.broadcasted_iota(jnp.int32, (N_HEADS, D_MODEL), 0)
        own_head = (lane // HEAD_DIM) == sub
        parts = []
        for t in range(t_new):
            row = jnp.broadcast_to(q_ref[t:t + 1, :], (N_HEADS, D_MODEL))
            parts.append(jnp.where(own_head, row * ATT_SCALE, 0.0))
        qbd[...] = jnp.concatenate(parts, axis=0).astype(BF16)
        acc[...] = jnp.zeros_like(acc)
        carry[...] = jnp.zeros_like(carry)
        knew[...] = jnp.zeros_like(knew)
        vnew[...] = jnp.zeros_like(vnew)
        knew[0:NEW_PAD, :] = kn_ref[...]
        vnew[0:NEW_PAD, :] = vn_ref[...]
        key = lax.broadcasted_iota(jnp.int32, (QROWS, page), 1)
        qt = lax.broadcasted_iota(jnp.int32, (QROWS, page), 0) // N_HEADS
        _sb_block(qbd[...], knew[...].astype(BF16), vnew[...].astype(BF16), bias, tri, key < qt, acc, carry)

    for i in range(npg):
        _sb_block(qbd[...], kp[i][...].astype(BF16), vp[i][...].astype(BF16), bias, tri, None, acc, carry)

    @pl.when(s == pl.num_programs(1) - 1)
    def _():
        lane = lax.broadcasted_iota(jnp.int32, (QROWS, D_MODEL), 1)
        sub = lax.broadcasted_iota(jnp.int32, (QROWS, D_MODEL), 0)
        own = jnp.where((lane // HEAD_DIM) == (sub % N_HEADS), acc[...], 0.0)
        o_ref[...] = jnp.zeros_like(o_ref)
        for t in range(t_new):
            o_ref[t:t + 1, :] = jnp.sum(own[t * N_HEADS:(t + 1) * N_HEADS, :], axis=0, keepdims=True)


def _attn_decode_call(page_table, bias_rows, q8, kn8, vn8, cache_k, cache_v, layer, t_new):
    nb, n_pages = page_table.shape
    depth, n_pool, page = cache_k.shape[:3]
    ck = cache_k.reshape(depth * n_pool, page, D_MODEL)
    cv = cache_v.reshape(depth * n_pool, page, D_MODEL)
    npg = PAGES_PER_STEP
    n_steps = n_pages // npg
    base = layer * n_pool

    def page_spec(i):
        def imap(b, s, pt):
            return (base + pt[b * n_pages + (n_pages - 1 - (s * npg + i))], 0, 0)
        return pl.BlockSpec((None, page, D_MODEL), imap)

    small = pl.BlockSpec((None, NEW_PAD, D_MODEL), lambda b, s, pt: (b, 0, 0))
    kern = functools.partial(_attn_decode_kernel, page=page, t_new=t_new)
    grid_spec = pltpu.PrefetchScalarGridSpec(
        num_scalar_prefetch=1,
        grid=(nb, n_steps),
        in_specs=[pl.BlockSpec((QROWS, 1), lambda b, s, pt: (0, 0)), small, small, small]
        + [page_spec(i) for i in range(npg)] + [page_spec(i) for i in range(npg)],
        out_specs=pl.BlockSpec((None, NEW_PAD, D_MODEL), lambda b, s, pt: (b, 0, 0)),
        scratch_shapes=[
            pltpu.VMEM((QROWS, D_MODEL), BF16),
            pltpu.VMEM((page, D_MODEL), F32),
            pltpu.VMEM((page, D_MODEL), F32),
            pltpu.VMEM((QROWS, D_MODEL), F32),
            pltpu.VMEM((QROWS, 1), F32),
        ],
    )
    return pl.pallas_call(
        kern,
        grid_spec=grid_spec,
        out_shape=jax.ShapeDtypeStruct((nb, NEW_PAD, D_MODEL), F32),
        compiler_params=_cparams(("arbitrary", "arbitrary"), 40),
        name="attn_decode",
    )(page_table.reshape(-1), bias_rows, q8, kn8, vn8, *([ck] * npg), *([cv] * npg))


def _merge_kernel(hg_ref, o_ref, ga_ref, gb_ref, x_ref, gate_ref, wa_ref, wb_ref, wo_ref, x1_ref):
    out_a = jnp.dot(hg_ref[...], wa_ref[...], preferred_element_type=F32)
    out_b = jnp.dot(o_ref[...], wb_ref[...], preferred_element_type=F32)
    m = _sigmoid(ga_ref[...]) * out_a + _sigmoid(gb_ref[...]) * out_b
    merged = jnp.dot(m.astype(BF16), wo_ref[...], preferred_element_type=F32)
    x1_ref[...] = x_ref[...] + gate_ref[...] * merged


def _merge_call(hg, o, proj, x, gate1, w_a, w_b, w_o, tm, tiles_per_row):
    n = x.shape[0]
    row = pl.BlockSpec((tm, D_MODEL), lambda i: (i, 0))
    wspec = pl.BlockSpec((D_MODEL, D_MODEL), lambda i: (0, 0))
    return pl.pallas_call(
        _merge_kernel,
        grid=(n // tm,),
        in_specs=[
            row, row,
            pl.BlockSpec((None, tm, D_MODEL), lambda i: (5, i, 0)),
            pl.BlockSpec((None, tm, D_MODEL), lambda i: (6, i, 0)),
            row,
            _mod_spec(gate1, tiles_per_row),
            wspec, wspec, wspec,
        ],
        out_specs=row,
        out_shape=jax.ShapeDtypeStruct((n, D_MODEL), F32),
        compiler_params=_cparams(("arbitrary",), 48),
        name="merge",
    )(hg, o, proj, proj, x, gate1, w_a, w_b, w_o)


INFO_GROUP = 0
INFO_RANK = 1
INFO_GATE0 = 4


def _router_kernel(x1_ref, shift_ref, scale_ref, g_ref, whi_ref, wlo_ref, br_ref,
                   h2_ref, info_ref, cnt_ref, carry, *, tm):
    @pl.when(pl.program_id(0) == 0)
    def _():
        carry[...] = jnp.zeros_like(carry)

    h2 = _rms_mod(x1_ref[...], g_ref[...], shift_ref[...], scale_ref[...])
    h2_ref[...] = h2
    hi = h2.astype(BF16)
    lo = (h2 - hi.astype(F32)).astype(BF16)
    logits = (jnp.dot(hi, whi_ref[...], preferred_element_type=F32)
              + jnp.dot(lo, whi_ref[...], preferred_element_type=F32)
              + jnp.dot(hi, wlo_ref[...], preferred_element_type=F32)) + br_ref[...]
    lane = lax.broadcasted_iota(jnp.int32, (tm, LANES), 1).astype(F32)

    def first_argmax(v):
        m = jnp.max(v, axis=-1, keepdims=True)
        idx = jnp.min(jnp.where(v == m, lane, float(LANES)), axis=-1, keepdims=True)
        return m, idx

    is_group = lane < float(N_GROUPS)
    lg = jnp.where(is_group, logits, NEG_BIG)
    mg, gidx = first_argmax(lg)
    p_top = 1.0 / jnp.sum(jnp.where(is_group, jnp.exp(lg - mg), 0.0), axis=-1, keepdims=True)
    first = float(INFO_GATE0) + float(EXPERTS_PER_GROUP) * gidx
    le = jnp.where(lane >= first, jnp.where(lane < first + float(EXPERTS_PER_GROUP), logits, NEG_BIG), NEG_BIG)
    m1, i1 = first_argmax(le)
    le2 = jnp.where(lane == i1, NEG_BIG, le)
    m2, i2 = first_argmax(le2)
    e21 = jnp.exp(m2 - m1)
    w1 = p_top / (1.0 + e21)
    w2 = p_top * e21 / (1.0 + e21)
    gates = jnp.where(lane == i1, w1, 0.0) + jnp.where(lane == i2, w2, 0.0)
    onehot = jnp.where(lane == gidx, 1.0, 0.0)
    r = lax.broadcasted_iota(jnp.int32, (tm, tm), 0)
    c = lax.broadcasted_iota(jnp.int32, (tm, tm), 1)
    before = jnp.where(c < r, 1.0, 0.0).astype(BF16)
    ranks = jnp.dot(before, onehot.astype(BF16), preferred_element_type=F32) + carry[...]
    rank = jnp.sum(onehot * ranks, axis=-1, keepdims=True)
    carry[...] += jnp.sum(onehot, axis=0, keepdims=True)
    info_ref[...] = jnp.where(lane == float(INFO_GROUP), gidx, jnp.where(lane == float(INFO_RANK), rank, gates))
    cnt_ref[...] = jnp.broadcast_to(carry[...], cnt_ref.shape)


def _router_call(x1, shift, scale, g, whi, wlo, br, tm, tiles_per_row):
    n = x1.shape[0]
    row = pl.BlockSpec((tm, D_MODEL), lambda i: (i, 0))
    wspec = pl.BlockSpec((D_MODEL, LANES), lambda i: (0, 0))
    kern = functools.partial(_router_kernel, tm=tm)
    return pl.pallas_call(
        kern,
        grid=(n // tm,),
        in_specs=[
            row,
            _mod_spec(shift, tiles_per_row),
            _mod_spec(scale, tiles_per_row),
            pl.BlockSpec((1, D_MODEL), lambda i: (0, 0)),
            wspec, wspec,
            pl.BlockSpec((1, LANES), lambda i: (0, 0)),
        ],
        out_specs=[
            row,
            pl.BlockSpec((tm, LANES), lambda i: (i, 0)),
            pl.BlockSpec((8, LANES), lambda i: (0, 0)),
        ],
        out_shape=[
            jax.ShapeDtypeStruct((n, D_MODEL), F32),
            jax.ShapeDtypeStruct((n, LANES), F32),
            jax.ShapeDtypeStruct((8, LANES), F32),
        ],
        scratch_shapes=[pltpu.VMEM((1, LANES), F32)],
        compiler_params=_cparams(("arbitrary",), 32),
        name="router",
    )(x1, shift, scale, g, whi, wlo, br)


def _row_gather_copy(src_hbm, idx, buf, slot, r, sem):
    return pltpu.make_async_copy(src_hbm.at[pl.ds(idx, 1), :], buf.at[slot, pl.ds(r, 1), :], sem.at[slot])


def _issue_row_gather(idx_ref, base, src_hbm, buf, slot, sem, tm):
    def body(r, c):
        _row_gather_copy(src_hbm, idx_ref[base + r], buf, slot, r, sem).start()
        return c

    lax.fori_loop(0, tm, body, 0)


def _wait_row_gather(src_hbm, buf, slot, sem, tm):
    pltpu.make_async_copy(src_hbm.at[pl.ds(0, tm), :], buf.at[slot], sem.at[slot]).wait()


def _expert_kernel(tg_ref, nv_ref, src_ref, h2_hbm, gs_ref, wg_ref, wu_ref, wd_ref, y_ref, xbuf, sem, *, tm):
    t = pl.program_id(0)
    nv = nv_ref[0]

    @pl.when(t == 0)
    def _():
        _issue_row_gather(src_ref, 0, h2_hbm, xbuf, 0, sem, tm)

    @pl.when(t + 1 < nv)
    def _():
        _issue_row_gather(src_ref, (t + 1) * tm, h2_hbm, xbuf, (t + 1) % 2, sem, tm)

    @pl.when(t < nv)
    def _():
        slot = t % 2
        _wait_row_gather(h2_hbm, xbuf, slot, sem, tm)
        xb = xbuf[slot].astype(BF16)
        gs = gs_ref[...]
        lane = lax.broadcasted_iota(jnp.int32, (tm, LANES), 1)
        first = INFO_GATE0 + EXPERTS_PER_GROUP * tg_ref[t]
        y = jnp.zeros((tm, D_MODEL), F32)
        for j in range(EXPERTS_PER_GROUP):
            gate = jnp.sum(jnp.where(lane == first + j, gs, 0.0), axis=-1, keepdims=True)
            hg = jnp.dot(xb, wg_ref[j], preferred_element_type=F32)
            hu = jnp.dot(xb, wu_ref[j], preferred_element_type=F32)
            hdn = (hg * _sigmoid(hg)) * hu * gate
            y = y + jnp.dot(hdn.astype(BF16), wd_ref[j], preferred_element_type=F32)
        y_ref[...] = y

    @pl.when(t >= nv)
    def _():
        y_ref[...] = jnp.zeros_like(y_ref)


def _expert_call(tile_group, n_valid, src, h2, gs, w_gate, w_up, w_down, tm):
    p = src.shape[0]
    kern = functools.partial(_expert_kernel, tm=tm)
    wspec = lambda shp: pl.BlockSpec((None,) + shp, lambda t, tg, nv, sr: (tg[t], 0, 0, 0))
    grid_spec = pltpu.PrefetchScalarGridSpec(
        num_scalar_prefetch=3,
        grid=(p // tm,),
        in_specs=[
            pl.BlockSpec(memory_space=pl.ANY),
            pl.BlockSpec((tm, LANES), lambda t, tg, nv, sr: (t, 0)),
            wspec((EXPERTS_PER_GROUP, D_MODEL, D_EXPERT)),
            wspec((EXPERTS_PER_GROUP, D_MODEL, D_EXPERT)),
            wspec((EXPERTS_PER_GROUP, D_EXPERT, D_MODEL)),
        ],
        out_specs=pl.BlockSpec((tm, D_MODEL), lambda t, tg, nv, sr: (t, 0)),
        scratch_shapes=[pltpu.VMEM((2, tm, D_MODEL), F32), pltpu.SemaphoreType.DMA((2,))],
    )
    return pl.pallas_call(
        kern,
        grid_spec=grid_spec,
        out_shape=jax.ShapeDtypeStruct((p, D_MODEL), F32),
        compiler_params=_cparams(("arbitrary",), 56),
        name="experts",
    )(tile_group, n_valid, src, h2, gs, w_gate, w_up, w_down)


def _combine_kernel(pos_ref, x1_ref, gate_ref, fg_ref, ys_hbm, o_ref, ybuf, sem, *, tm, final):
    i = pl.program_id(0)

    @pl.when(i == 0)
    def _():
        _issue_row_gather(pos_ref, 0, ys_hbm, ybuf, 0, sem, tm)

    @pl.when(i + 1 < pl.num_programs(0))
    def _():
        _issue_row_gather(pos_ref, (i + 1) * tm, ys_hbm, ybuf, (i + 1) % 2, sem, tm)

    slot = i % 2
    _wait_row_gather(ys_hbm, ybuf, slot, sem, tm)
    x2 = x1_ref[...] + gate_ref[...] * ybuf[slot]
    if final:
        ms = jnp.mean(x2 * x2, axis=-1, keepdims=True)
        x2 = x2 * lax.rsqrt(ms + EPS) * fg_ref[...]
    o_ref[...] = x2


def _combine_call(pos, x1, gate2, final_g, ys, tm, tiles_per_row, final):
    n = x1.shape[0]
    kern = functools.partial(_combine_kernel, tm=tm, final=final)
    row = pl.BlockSpec((tm, D_MODEL), lambda i, ps: (i, 0))
    grid_spec = pltpu.PrefetchScalarGridSpec(
        num_scalar_prefetch=1,
        grid=(n // tm,),
        in_specs=[
            row,
            _mod_spec(gate2, tiles_per_row),
            pl.BlockSpec((1, D_MODEL), lambda i, ps: (0, 0)),
            pl.BlockSpec(memory_space=pl.ANY),
        ],
        out_specs=row,
        scratch_shapes=[pltpu.VMEM((2, tm, D_MODEL), F32), pltpu.SemaphoreType.DMA((2,))],
    )
    return pl.pallas_call(
        kern,
        grid_spec=grid_spec,
        out_shape=jax.ShapeDtypeStruct((n, D_MODEL), F32),
        compiler_params=_cparams(("arbitrary",), 32),
        name="combine",
    )(pos, x1, gate2, final_g, ys)


def _moe(x1, shift2, scale2, gate2, lw, final_g, tm_row, tiles_per_row, tm_e, final):
    n = x1.shape[0]
    h2, info, cnt = _router_call(x1, shift2, scale2, lw["norm2_g"], lw["wr_hi"], lw["wr_lo"], lw["b_r"],
                                 tm_row, tiles_per_row)
    grp = info[:, INFO_GROUP].astype(jnp.int32)
    rank = info[:, INFO_RANK].astype(jnp.int32)
    counts = cnt[0, :N_GROUPS].astype(jnp.int32)
    tiles_g = (counts + tm_e - 1) // tm_e
    tile_end = jnp.cumsum(tiles_g)
    offs = (tile_end - tiles_g) * tm_e
    pos = offs[grp] + rank
    p_rows = (-(-n // tm_e) + N_GROUPS) * tm_e
    src = jnp.zeros((p_rows,), jnp.int32).at[pos].set(jnp.arange(n, dtype=jnp.int32))
    tile_ids = jnp.arange(p_rows // tm_e, dtype=jnp.int32)
    tile_group = jnp.minimum(jnp.sum(tile_ids[:, None] >= tile_end[None, :], axis=1), N_GROUPS - 1).astype(jnp.int32)
    n_valid = tile_end[-1:].astype(jnp.int32)
    gs = info[src]
    ys = _expert_call(tile_group, n_valid, src, h2, gs, lw["w_gate"], lw["w_up"], lw["w_down"], tm_e)
    return _combine_call(pos, x1, gate2, final_g, ys, tm_row, tiles_per_row, final)


def _layer_weights(l, w_in, norm1_g, norm2_g, b_sb, conv_w, conv_b, w_rg_a, b_rg_a, w_rg_i, b_rg_i, lru_lambda,
                   w_branch_a, w_branch_b, w_out, w_router_group, b_router_group, w_router_expert,
                   b_router_expert, w_exp_gate, w_exp_up, w_exp_down):
    pad = LANES - INFO_GATE0 - N_EXPERTS
    w_r = jnp.concatenate([w_router_group[l], w_router_expert[l], jnp.zeros((D_MODEL, pad), F32)], axis=1)
    b_r = jnp.concatenate([b_router_group[l], b_router_expert[l], jnp.zeros((pad,), F32)])[None, :]
    wr_hi = w_r.astype(BF16)
    grouped = lambda w: w[l].astype(BF16).reshape((N_GROUPS, EXPERTS_PER_GROUP) + w.shape[2:])
    return {
        "w_in": w_in[l].astype(BF16),
        "norm1_g": norm1_g[l][None, :], "norm2_g": norm2_g[l][None, :],
        "b_sb": b_sb[l],
        "conv_w": conv_w[l], "conv_b": conv_b[l][None, :],
        "wai": jnp.concatenate([w_rg_a[l], w_rg_i[l]], axis=-1).astype(BF16),
        "b_a": b_rg_a[l][None, :], "b_i": b_rg_i[l][None, :], "lam": lru_lambda[l][None, :],
        "w_a": w_branch_a[l].astype(BF16), "w_b": w_branch_b[l].astype(BF16), "w_o": w_out[l].astype(BF16),
        "wr_hi": wr_hi, "wr_lo": (w_r - wr_hi.astype(F32)).astype(BF16), "b_r": b_r,
        "w_gate": grouped(w_exp_gate), "w_up": grouped(w_exp_up), "w_down": grouped(w_exp_down),
    }


def kernel(x_prompt, x_sample, c_prompt, c_sample, cache_k, cache_v, state_h, state_conv, page_table, w_mod, b_mod, norm1_g, norm2_g, w_in, b_sb, conv_w, conv_b, w_rg_a, b_rg_a, w_rg_i, b_rg_i, lru_lambda, w_branch_a, w_branch_b, w_out, w_router_group, b_router_group, w_router_expert, b_router_expert, w_exp_gate, w_exp_up, w_exp_down, final_g):
    bp, tp, _ = x_prompt.shape
    bs, ts, _ = x_sample.shape
    depth = w_in.shape[0]
    np_rows, ns_rows = bp * tp, bs * ts
    tm_p = 1024
    tpr_p = tp // tm_p
    tm_s = bs
    fg = final_g[None, :]

    xp = x_prompt.reshape(np_rows, D_MODEL)
    xs = jnp.transpose(x_sample, (1, 0, 2)).reshape(ns_rows, D_MODEL)
    c_all = jnp.concatenate([c_prompt, c_sample], axis=0)
    bias_rows_of = lambda b: jnp.tile(b, ts).reshape(QROWS, 1)
    to_bmajor8 = lambda a: jnp.pad(jnp.transpose(a.reshape(ts, bs, D_MODEL), (1, 0, 2)),
                                   ((0, 0), (0, NEW_PAD - ts), (0, 0)))

    outs = {k: [] for k in ("kp", "vp", "hp", "cp", "ks", "vs", "hs", "cs")}
    for l in range(depth):
        lw = _layer_weights(l, w_in, norm1_g, norm2_g, b_sb, conv_w, conv_b, w_rg_a, b_rg_a, w_rg_i, b_rg_i,
                            lru_lambda, w_branch_a, w_branch_b, w_out, w_router_group, b_router_group,
                            w_router_expert, b_router_expert, w_exp_gate, w_exp_up, w_exp_down)
        final = l == depth - 1
        mod = _mod_call(c_all, w_mod[l].astype(BF16), b_mod[l][None, :])
        mod_p = mod[:bp].reshape(bp, N_MOD, 1, D_MODEL)
        mod_s = mod[bp:].reshape(1, bs, N_MOD, D_MODEL)
        mp = [mod_p[:, i] for i in range(N_MOD)]
        ms = [mod_s[:, :, i] for i in range(N_MOD)]

        proj = _inproj_call(xp, mp[0], mp[1], lw["norm1_g"], lw["w_in"], tm_p, tpr_p)
        hg, h_last = _rglru_prompt_call(proj, lw["conv_w"], lw["conv_b"], lw["wai"], lw["b_a"], lw["b_i"],
                                        lw["lam"], bp, tp, 128)
        o = _attn_prompt_call(proj, lw["b_sb"], bp, tp, 256)
        x1 = _merge_call(hg.reshape(np_rows, D_MODEL), o.reshape(np_rows, D_MODEL), proj, xp, mp[2],
                         lw["w_a"], lw["w_b"], lw["w_o"], tm_p // 2, tpr_p * 2)
        xp = _moe(x1, mp[3], mp[4], mp[5], lw, fg, tm_p // 2, tpr_p * 2, 512, final)
        outs["kp"].append(proj[3].reshape(bp, tp, N_HEADS, HEAD_DIM))
        outs["vp"].append(proj[4].reshape(bp, tp, N_HEADS, HEAD_DIM))
        outs["hp"].append(h_last)
        outs["cp"].append(proj[0].reshape(bp, tp, D_MODEL)[:, tp - (CONV_WIDTH - 1):])

        proj_s = _inproj_call(xs, ms[0], ms[1], lw["norm1_g"], lw["w_in"], tm_s, 1)
        cs_tm = jnp.transpose(state_conv[l], (1, 0, 2))
        hg_s, h_last_s = _rglru_sample_call(proj_s, cs_tm, state_h[l], lw["conv_w"], lw["conv_b"], lw["wai"],
                                            lw["b_a"], lw["b_i"], lw["lam"], bs, ts)
        o8 = _attn_decode_call(page_table, bias_rows_of(lw["b_sb"]), to_bmajor8(proj_s[2]), to_bmajor8(proj_s[3]),
                               to_bmajor8(proj_s[4]), cache_k, cache_v, l, ts)
        o_s = jnp.transpose(o8[:, :ts], (1, 0, 2)).reshape(ns_rows, D_MODEL).astype(BF16)
        x1_s = _merge_call(hg_s.reshape(ns_rows, D_MODEL), o_s, proj_s, xs, ms[2],
                           lw["w_a"], lw["w_b"], lw["w_o"], tm_s, 1)
        xs = _moe(x1_s, ms[3], ms[4], ms[5], lw, fg, tm_s, 1, 128, final)
        tm_to_b = lambda a: jnp.transpose(a.reshape(ts, bs, D_MODEL), (1, 0, 2))
        outs["ks"].append(tm_to_b(proj_s[3]).reshape(bs, ts, N_HEADS, HEAD_DIM))
        outs["vs"].append(tm_to_b(proj_s[4]).reshape(bs, ts, N_HEADS, HEAD_DIM))
        outs["hs"].append(h_last_s)
        xc_tail = jnp.concatenate([cs_tm, proj_s[0].reshape(ts, bs, D_MODEL)], axis=0)[-(CONV_WIDTH - 1):]
        outs["cs"].append(jnp.transpose(xc_tail, (1, 0, 2)))

    y_prompt = xp.reshape(bp, tp, D_MODEL)
    y_sample = jnp.transpose(xs.reshape(ts, bs, D_MODEL), (1, 0, 2))
    st = lambda k: jnp.stack(outs[k])
    return (y_prompt, y_sample, st("kp"), st("vp"), st("hp"), st("cp"), st("ks"), st("vs"), st("hs"), st("cs"))
```
